```python
import math
import jax, jax.numpy as jnp
from jax import lax
import numpy as np

D_MODEL = 4096
BATCH = 4
SEQ = 2048
DEPTH = 2
DEC_BATCH = 8
DEC_SEQ = 1
PAST_LEN = 16384
PAGE_SIZE = 128

A_GROUPS = ((128, 1), (512, 4), (2048, 16))
N_A_GROUPS = len(A_GROUPS)
DH_A = 128
A_WIDTH = D_MODEL // 4
H_A = A_WIDTH // DH_A
A_BLOCK = 128
N_BUCKETS = 32
MAX_EXACT = N_BUCKETS // 2
BUCKET_MAX_DIST = 2048
DK_R = 256
DV_R = 256
R_WIDTH = D_MODEL // 2
H_R = R_WIDTH // DV_R
RET_CHUNK = 128
ROPE_BASE = 10000.0
N_MEM = 256
H_C = 4
C_WIDTH = D_MODEL // 4
DH_C = C_WIDTH // H_C
D_FF = ((8 * D_MODEL // 3 + 255) // 256) * 256
N_BRANCH = 3
A_COLS = N_A_GROUPS * 3 * A_WIDTH
R_COLS = 2 * H_R * DK_R + 2 * R_WIDTH
C_COLS = C_WIDTH
G_COLS = N_BRANCH * D_MODEL
N_IN = A_COLS + R_COLS + C_COLS + G_COLS
EPS = 1e-6
NEG_INF = -1e30

kernel_name = 'hybrid_dilated_retention_memory_decoder_step'


def rmsnorm(x, g):
    xf = x.astype(jnp.float32)
    return xf * lax.rsqrt(jnp.mean(xf * xf, axis=-1, keepdims=True) + EPS) * g


def head_rmsnorm(x, g):
    return x * lax.rsqrt(jnp.mean(x * x, axis=-1, keepdims=True) + EPS) * g.reshape(H_R, DV_R)


def swiglu(h, w_gate, w_up, w_down):
    return (jax.nn.silu(h @ w_gate) * (h @ w_up)) @ w_down


def t5_bucket(dist):
    dist = dist.astype(jnp.int32)
    large = MAX_EXACT + (jnp.log(jnp.maximum(dist, MAX_EXACT).astype(jnp.float32) / MAX_EXACT)
                         / math.log(BUCKET_MAX_DIST / MAX_EXACT) * (N_BUCKETS - MAX_EXACT)).astype(jnp.int32)
    return jnp.where(dist < MAX_EXACT, dist, jnp.minimum(large, N_BUCKETS - 1))


def rotary(x, pos):
    d = x.shape[-1]
    inv_freq = 1.0 / (ROPE_BASE ** jnp.linspace(0.0, 1.0, d // 2, dtype=jnp.float32))
    ang = pos[:, None] * inv_freq[None, :]
    cos = jnp.cos(ang)[None, :, None, :]
    sin = jnp.sin(ang)[None, :, None, :]
    x1 = x[..., 0::2]
    x2 = x[..., 1::2]
    return jnp.stack([x1 * cos - x2 * sin, x1 * sin + x2 * cos], axis=-1).reshape(x.shape)


def dilated_group_prompt(q, k, v, window, dilation, bias_tab):
    b, s, h, dh = q.shape
    nk = window // dilation
    n_sub = s // dilation
    nb = -(-n_sub // A_BLOCK)
    lp = nb * A_BLOCK

    def strided(x):
        x = x.astype(jnp.float32).reshape(b, n_sub, dilation, h, dh)
        return jnp.pad(x, ((0, 0), (0, lp - n_sub), (0, 0), (0, 0), (0, 0)))

    def band(x):
        xp = jnp.pad(strided(x), ((0, 0), (A_BLOCK, 0), (0, 0), (0, 0), (0, 0)))
        xp = xp.reshape(b, nb + 1, A_BLOCK, dilation, h, dh)
        return jnp.concatenate([xp[:, :-1], xp[:, 1:]], axis=2)

    qb = strided(q).reshape(b, nb, A_BLOCK, dilation, h, dh)
    kb = band(k)
    vb = band(v)
    qi = jnp.arange(A_BLOCK)[:, None]
    ki = jnp.arange(2 * A_BLOCK)[None, :]
    j = qi + A_BLOCK - ki
    key_sub = jnp.arange(nb)[:, None, None] * A_BLOCK + ki[None] - A_BLOCK
    valid = ((j >= 0) & (j <= nk))[None] & (key_sub >= 0)
    bias = jnp.transpose(bias_tab[t5_bucket(jnp.clip(j, 0, nk) * dilation)], (2, 0, 1)).astype(jnp.float32)
    logits = jnp.einsum('bnqrhd,bnkrhd->bnrhqk', qb, kb) + bias
    logits = jnp.where(valid[None, :, None, None], logits, NEG_INF)
    m = jnp.max(logits, axis=-1, keepdims=True)
    p = jnp.exp(logits - m)
    den = jnp.transpose(jnp.sum(p, axis=-1), (0, 1, 4, 2, 3))
    o = jnp.einsum('bnrhqk,bnkrhd->bnqrhd', p, vb) / den[..., None]

    def back(y):
        rest = y.shape[4:]
        return y.reshape((b, lp, dilation) + rest)[:, :n_sub].reshape((b, s) + rest)

    return back(o), back(jnp.transpose(m[..., 0], (0, 1, 4, 2, 3))), back(den)


def dilated_group_decode(q, k, v, buf, window, dilation, bias_tab):
    b, t, h, dh = q.shape
    lg = buf.shape[1]
    nk = window // dilation
    kcat = jnp.concatenate([buf[:, :, 0].astype(jnp.float32), k.astype(jnp.float32)], axis=1)
    vcat = jnp.concatenate([buf[:, :, 1].astype(jnp.float32), v.astype(jnp.float32)], axis=1)
    j = jnp.arange(nk + 1)
    idx = lg + jnp.arange(t)[:, None] - dilation * j[None, :]
    valid = idx >= 0
    idx = jnp.maximum(idx, 0)
    kg = kcat[:, idx]
    vg = vcat[:, idx]
    bias = bias_tab[t5_bucket(j * dilation)].T.astype(jnp.float32)
    logits = jnp.einsum('bthd,btjhd->bhtj', q.astype(jnp.float32), kg) + bias[None, :, None, :]
    logits = jnp.where(valid[None, None], logits, NEG_INF)
    m = jnp.max(logits, axis=-1, keepdims=True)
    p = jnp.exp(logits - m)
    den = jnp.transpose(jnp.sum(p, axis=-1), (0, 2, 1))
    o = jnp.einsum('bhtj,btjhd->bthd', p, vg) / den[..., None]
    new_buf = jnp.stack([kcat[:, t:], vcat[:, t:]], axis=2)
    return o, jnp.transpose(m[..., 0], (0, 2, 1)), den, new_buf


def merge_groups(outs, ms, dens):
    o = jnp.stack(outs)
    m = jnp.stack(ms)
    den = jnp.stack(dens)
    w = jnp.exp(m - jnp.max(m, axis=0, keepdims=True)) * den
    return jnp.sum(w[..., None] * o, axis=0) / jnp.sum(w, axis=0)[..., None]


def retention(q, k, v, s0, log_gamma):
    b, t, h, dk = q.shape
    dv = v.shape[-1]
    c = RET_CHUNK if t % RET_CHUNK == 0 else t
    n = t // c
    q = q.reshape(b, n, c, h, dk)
    k = k.reshape(b, n, c, h, dk)
    v = v.astype(jnp.float32).reshape(b, n, c, h, dv)
    i = jnp.arange(c, dtype=jnp.float32)
    diff = i[:, None] - i[None, :]
    decay = jnp.where(diff >= 0, jnp.exp(jnp.maximum(diff, 0.0)[None] * log_gamma[:, None, None]), 0.0)
    scores = jnp.einsum('bnihd,bnjhd->bnhij', q, k) * decay
    o_intra = jnp.einsum('bnhij,bnjhe->bnihe', scores, v)
    k_dec = k * jnp.exp((c - 1 - i)[:, None] * log_gamma[None, :])[None, None, :, :, None]
    kv = jnp.einsum('bnjhd,bnjhe->nbhde', k_dec, v)
    chunk_decay = jnp.exp(c * log_gamma)[None, :, None, None]

    def step(s, kv_n):
        return chunk_decay * s + kv_n, s

    s_final, s_before = lax.scan(step, s0.astype(jnp.float32), kv)
    q_dec = q * jnp.exp((i + 1.0)[:, None] * log_gamma[None, :])[None, None, :, :, None]
    o_inter = jnp.einsum('bnihd,nbhde->bnihe', q_dec, s_before)
    return (o_intra + o_inter).reshape(b, t, h, dv), s_final


def memory_kv(mem, g, w):
    return (rmsnorm(mem, g) @ w).reshape(mem.shape[0], mem.shape[1], 2, H_C, DH_C)


def mem_attention(q, mem_kv):
    kv = mem_kv.astype(jnp.float32)
    p = jax.nn.softmax(jnp.einsum('bthd,bmhd->bhtm', q.astype(jnp.float32), kv[:, :, 0]), axis=-1)
    return jnp.einsum('bhtm,bmhd->bthd', p, kv[:, :, 1])


def token_mixing(h, pos, bufs, s0, mem_kv, p, l):
    b, t, _ = h.shape
    proj = h @ p['w_in'][l]
    a_cols, r_cols, c_cols, g_cols = jnp.split(proj, [A_COLS, A_COLS + R_COLS, A_COLS + R_COLS + C_COLS], axis=-1)
    a = a_cols.reshape(b, t, N_A_GROUPS, 3, H_A, DH_A)
    outs, ms, dens, new_bufs = [], [], [], []
    for gi, (window, dilation) in enumerate(A_GROUPS):
        q = a[:, :, gi, 0] * DH_A ** -0.5
        k = a[:, :, gi, 1]
        v = a[:, :, gi, 2]
        bias_tab = p['rel_bias'][:, gi * H_A:(gi + 1) * H_A]
        if bufs is None:
            o, m, den = dilated_group_prompt(q, k, v, window, dilation, bias_tab)
            keep = min(window, t)
            nbuf = jnp.stack([k[:, t - keep:], v[:, t - keep:]], axis=2)
        else:
            o, m, den, nbuf = dilated_group_decode(q, k, v, bufs[gi], window, dilation, bias_tab)
        outs.append(o)
        ms.append(m)
        dens.append(den)
        new_bufs.append(nbuf)
    o_a = merge_groups(outs, ms, dens).reshape(b, t, A_WIDTH)
    qr, kr, vr, gr = jnp.split(r_cols.astype(jnp.float32), [H_R * DK_R, 2 * H_R * DK_R, 2 * H_R * DK_R + R_WIDTH], axis=-1)
    qr = rotary(qr.reshape(b, t, H_R, DK_R), pos)
    kr = rotary(kr.reshape(b, t, H_R, DK_R), pos) * DK_R ** -0.5
    log_gamma = jnp.log1p(-jnp.exp2(-5.0 - jnp.arange(H_R, dtype=jnp.float32)))
    o_r, s_new = retention(qr, kr, vr.reshape(b, t, H_R, DV_R), s0, log_gamma)
    o_r = (jax.nn.silu(gr.reshape(b, t, H_R, DV_R)) * head_rmsnorm(o_r, p['ret_gn_g'][l])).reshape(b, t, R_WIDTH)
    qc = c_cols.reshape(b, t, H_C, DH_C) * DH_C ** -0.5
    o_c = mem_attention(qc, mem_kv).reshape(b, t, C_WIDTH)
    gates = jax.nn.sigmoid(g_cols.astype(jnp.float32).reshape(b, t, N_BRANCH, D_MODEL))
    merged = (gates[:, :, 0] * (o_a @ p['w_br_a'][l])
              + gates[:, :, 1] * (o_r @ p['w_br_r'][l])
              + gates[:, :, 2] * (o_c @ p['w_br_c'][l]))
    return merged @ p['w_out'][l], new_bufs, s_new


def run_trunk(x, pos0, p, a_caches, ret_cache, mem_kv_cache, mem):
    b, t, _ = x.shape
    pos = pos0 + jnp.arange(t, dtype=jnp.float32)
    new_a = [[] for _ in A_GROUPS]
    new_ret = []
    new_mem = []
    for l in range(DEPTH):
        h1 = rmsnorm(x, p['ffn1_pre_g'][l])
        x = x + 0.5 * rmsnorm(swiglu(h1, p['ffn1_w_gate'][l], p['ffn1_w_up'][l], p['ffn1_w_down'][l]), p['ffn1_post_g'][l])
        h = rmsnorm(x, p['mix_pre_g'][l])
        if mem is None:
            mem_kv = mem_kv_cache[l]
        else:
            mem_kv = memory_kv(mem, p['mem_norm_g'][l], p['w_mem_kv'][l])
            new_mem.append(mem_kv)
        bufs = None if a_caches is None else [c[l] for c in a_caches]
        s0 = jnp.zeros((b, H_R, DK_R, DV_R), jnp.float32) if ret_cache is None else ret_cache[l]
        mix, bufs_new, s_new = token_mixing(h, pos, bufs, s0, mem_kv, p, l)
        x = x + rmsnorm(mix, p['mix_post_g'][l])
        h2 = rmsnorm(x, p['ffn2_pre_g'][l])
        x = x + 0.5 * rmsnorm(swiglu(h2, p['ffn2_w_gate'][l], p['ffn2_w_up'][l], p['ffn2_w_down'][l]), p['ffn2_post_g'][l])
        for gi in range(N_A_GROUPS):
            new_a[gi].append(bufs_new[gi])
        new_ret.append(s_new)
    mem_out = jnp.stack(new_mem) if new_mem else None
    return x, [jnp.stack(a) for a in new_a], jnp.stack(new_ret), mem_out


def setup_inputs(seed: int = 0) -> dict:
    key = jax.random.key(seed)
    keys = iter(jax.random.split(key, 40))

    def normal(shape, scale):
        return scale * jax.random.normal(next(keys), shape, jnp.float32)

    def gain(shape):
        return 1.0 + 0.02 * jax.random.normal(next(keys), shape, jnp.float32)

    a_len = [min(w, PAST_LEN) for w, _ in A_GROUPS]
    return {
        'x_prompt': normal((BATCH, SEQ, D_MODEL), 1.0),
        'x_sample': normal((DEC_BATCH, DEC_SEQ, D_MODEL), 1.0),
        'cache_a1_kv': normal((DEPTH, DEC_BATCH, a_len[0], 2, H_A, DH_A), 1.0),
        'cache_a2_kv': normal((DEPTH, DEC_BATCH, a_len[1], 2, H_A, DH_A), 1.0),
        'cache_a3_kv': normal((DEPTH, DEC_BATCH, a_len[2], 2, H_A, DH_A), 1.0),
        'state_ret': normal((DEPTH, DEC_BATCH, H_R, DK_R, DV_R), 0.5),
        'cache_mem_kv': normal((DEPTH, DEC_BATCH, N_MEM, 2, H_C, DH_C), 1.0),
        'mem_prompt': normal((BATCH, N_MEM, D_MODEL), 1.0),
        'ffn1_pre_g': gain((DEPTH, D_MODEL)),
        'ffn1_w_gate': normal((DEPTH, D_MODEL, D_FF), D_MODEL ** -0.5),
        'ffn1_w_up': normal((DEPTH, D_MODEL, D_FF), D_MODEL ** -0.5),
        'ffn1_w_down': normal((DEPTH, D_FF, D_MODEL), D_FF ** -0.5),
        'ffn1_post_g': gain((DEPTH, D_MODEL)),
        'mix_pre_g': gain((DEPTH, D_MODEL)),
        'w_in': normal((DEPTH, D_MODEL, N_IN), D_MODEL ** -0.5),
        'rel_bias': normal((N_BUCKETS, N_A_GROUPS * H_A), 0.5),
        'ret_gn_g': gain((DEPTH, R_WIDTH)),
        'mem_norm_g': gain((DEPTH, D_MODEL)),
        'w_mem_kv': normal((DEPTH, D_MODEL, 2 * C_WIDTH), D_MODEL ** -0.5),
        'w_br_a': normal((DEPTH, A_WIDTH, D_MODEL), A_WIDTH ** -0.5),
        'w_br_r': normal((DEPTH, R_WIDTH, D_MODEL), R_WIDTH ** -0.5),
        'w_br_c': normal((DEPTH, C_WIDTH, D_MODEL), C_WIDTH ** -0.5),
        'w_out': normal((DEPTH, D_MODEL, D_MODEL), D_MODEL ** -0.5),
        'mix_post_g': gain((DEPTH, D_MODEL)),
        'ffn2_pre_g': gain((DEPTH, D_MODEL)),
        'ffn2_w_gate': normal((DEPTH, D_MODEL, D_FF), D_MODEL ** -0.5),
        'ffn2_w_up': normal((DEPTH, D_MODEL, D_FF), D_MODEL ** -0.5),
        'ffn2_w_down': normal((DEPTH, D_FF, D_MODEL), D_FF ** -0.5),
        'ffn2_post_g': gain((DEPTH, D_MODEL)),
    }


def reference(x_prompt, x_sample, cache_a1_kv, cache_a2_kv, cache_a3_kv, state_ret, cache_mem_kv, mem_prompt,
              ffn1_pre_g, ffn1_w_gate, ffn1_w_up, ffn1_w_down, ffn1_post_g, mix_pre_g, w_in, rel_bias, ret_gn_g,
              mem_norm_g, w_mem_kv, w_br_a, w_br_r, w_br_c, w_out, mix_post_g,
              ffn2_pre_g, ffn2_w_gate, ffn2_w_up, ffn2_w_down, ffn2_post_g):
    p = {
        'ffn1_pre_g': ffn1_pre_g, 'ffn1_w_gate': ffn1_w_gate, 'ffn1_w_up': ffn1_w_up,
        'ffn1_w_down': ffn1_w_down, 'ffn1_post_g': ffn1_post_g, 'mix_pre_g': mix_pre_g,
        'w_in': w_in, 'rel_bias': rel_bias, 'ret_gn_g': ret_gn_g, 'mem_norm_g': mem_norm_g,
        'w_mem_kv': w_mem_kv, 'w_br_a': w_br_a, 'w_br_r': w_br_r, 'w_br_c': w_br_c,
        'w_out': w_out, 'mix_post_g': mix_post_g, 'ffn2_pre_g': ffn2_pre_g,
        'ffn2_w_gate': ffn2_w_gate, 'ffn2_w_up': ffn2_w_up, 'ffn2_w_down': ffn2_w_down,
        'ffn2_post_g': ffn2_post_g,
    }
    y_prompt, a_p, ret_p, mem_p = run_trunk(x_prompt, 0, p, None, None, None, mem_prompt)
    y_sample, a_s, ret_s, _ = run_trunk(x_sample, PAST_LEN, p, (cache_a1_kv, cache_a2_kv, cache_a3_kv),
                                        state_ret, cache_mem_kv, None)
    return (y_prompt, y_sample, a_p[0], a_p[1], a_p[2], ret_p, mem_p, a_s[0], a_s[1], a_s[2], ret_s)
```

```python
import functools
import math

import numpy as np
import jax
import jax.numpy as jnp
from jax import lax
from jax.experimental import pallas as pl
from jax.experimental.pallas import tpu as pltpu

D_MODEL = 4096
DEPTH = 2
PAST_LEN = 16384
A_GROUPS = ((128, 1), (512, 4), (2048, 16))
N_A_GROUPS = len(A_GROUPS)
DH_A = 128
A_WIDTH = D_MODEL // 4
H_A = A_WIDTH // DH_A
A_BLOCK = 128
N_BUCKETS = 32
MAX_EXACT = N_BUCKETS // 2
BUCKET_MAX_DIST = 2048
DK_R = 256
DV_R = 256
R_WIDTH = D_MODEL // 2
H_R = R_WIDTH // DV_R
RET_CHUNK = 128
ROPE_BASE = 10000.0
N_MEM = 256
H_C = 4
C_WIDTH = D_MODEL // 4
DH_C = C_WIDTH // H_C
D_FF = ((8 * D_MODEL // 3 + 255) // 256) * 256
N_BRANCH = 3
A_COLS = N_A_GROUPS * 3 * A_WIDTH
R_COLS = 2 * H_R * DK_R + 2 * R_WIDTH
C_COLS = C_WIDTH
G_COLS = N_BRANCH * D_MODEL
N_IN = A_COLS + R_COLS + C_COLS + G_COLS
R_OFF = A_COLS
C_OFF = A_COLS + R_COLS
G_OFF = A_COLS + R_COLS + C_COLS
EPS = 1e-6
NEG_INF = -1e30

LANES = 128
VMEM_LIMIT_BYTES = 58 * 1024 * 1024
D_FF_PAD = ((D_FF + 1023) // 1024) * 1024
SAMPLE_ROWS = 16

_BF16 = jnp.bfloat16
_F32 = jnp.float32


def _params(*sem):
    return pltpu.CompilerParams(dimension_semantics=sem, vmem_limit_bytes=VMEM_LIMIT_BYTES)


def _dot(a, b):
    return jnp.dot(a, b, preferred_element_type=_F32)


def _dot_nt(a, b):
    return lax.dot_general(a, b, (((1,), (1,)), ((), ())), preferred_element_type=_F32)


def _dot_tn(a, b):
    return lax.dot_general(a, b, (((0,), (0,)), ((), ())), preferred_element_type=_F32)


def _rms(x, g):
    return x * lax.rsqrt(jnp.mean(x * x, axis=-1, keepdims=True) + EPS) * g


def _norm_kernel(x_ref, g_ref, h_ref):
    h_ref[...] = _rms(x_ref[...], g_ref[...]).astype(h_ref.dtype)


def _norm(x, g, tm):
    m, d = x.shape
    return pl.pallas_call(
        _norm_kernel,
        grid=(m // tm,),
        in_specs=[pl.BlockSpec((tm, d), lambda i: (i, 0)), pl.BlockSpec((1, d), lambda i: (0, 0))],
        out_specs=pl.BlockSpec((tm, d), lambda i: (i, 0)),
        out_shape=jax.ShapeDtypeStruct((m, d), _BF16),
        compiler_params=_params("parallel"),
    )(x, g.reshape(1, d))


def _resid_norm_kernel(x_ref, y_ref, gp_ref, gn_ref, xo_ref, h_ref, *, scale):
    x = x_ref[...] + scale * _rms(y_ref[...], gp_ref[...])
    xo_ref[...] = x
    h_ref[...] = _rms(x, gn_ref[...]).astype(h_ref.dtype)


def _resid_kernel(x_ref, y_ref, gp_ref, xo_ref, *, scale):
    xo_ref[...] = x_ref[...] + scale * _rms(y_ref[...], gp_ref[...])


def _resid_norm(x, y, g_post, scale, g_next, tm):
    m, d = x.shape
    row = pl.BlockSpec((tm, d), lambda i: (i, 0))
    vec = pl.BlockSpec((1, d), lambda i: (0, 0))
    if g_next is None:
        return pl.pallas_call(
            functools.partial(_resid_kernel, scale=scale),
            grid=(m // tm,), in_specs=[row, row, vec], out_specs=row,
            out_shape=jax.ShapeDtypeStruct((m, d), _F32),
            compiler_params=_params("parallel"),
        )(x, y, g_post.reshape(1, d)), None
    return pl.pallas_call(
        functools.partial(_resid_norm_kernel, scale=scale),
        grid=(m // tm,), in_specs=[row, row, vec, vec], out_specs=[row, row],
        out_shape=[jax.ShapeDtypeStruct((m, d), _F32), jax.ShapeDtypeStruct((m, d), _BF16)],
        compiler_params=_params("parallel"),
    )(x, y, g_post.reshape(1, d), g_next.reshape(1, d))


def _mm_kernel(a_ref, b_ref, o_ref):
    o_ref[...] = _dot(a_ref[...], b_ref[...]).astype(o_ref.dtype)


def _mm_acc_kernel(a_ref, b_ref, o_ref, acc_ref, *, nk):
    k = pl.program_id(2)

    @pl.when(k == 0)
    def _():
        acc_ref[...] = jnp.zeros_like(acc_ref)

    acc_ref[...] += _dot(a_ref[...], b_ref[...])

    @pl.when(k == nk - 1)
    def _():
        o_ref[...] = acc_ref[...].astype(o_ref.dtype)


def _matmul(a, b, out_dtype, tm, tn, tk):
    m, kd = a.shape
    n = b.shape[1]
    nk = kd // tk
    if nk == 1:
        return pl.pallas_call(
            _mm_kernel,
            grid=(m // tm, n // tn),
            in_specs=[pl.BlockSpec((tm, kd), lambda i, j: (i, 0)), pl.BlockSpec((kd, tn), lambda i, j: (0, j))],
            out_specs=pl.BlockSpec((tm, tn), lambda i, j: (i, j)),
            out_shape=jax.ShapeDtypeStruct((m, n), out_dtype),
            compiler_params=_params("parallel", "parallel"),
        )(a, b)
    return pl.pallas_call(
        functools.partial(_mm_acc_kernel, nk=nk),
        grid=(m // tm, n // tn, nk),
        in_specs=[pl.BlockSpec((tm, tk), lambda i, j, k: (i, k)), pl.BlockSpec((tk, tn), lambda i, j, k: (k, j))],
        out_specs=pl.BlockSpec((tm, tn), lambda i, j, k: (i, j)),
        out_shape=jax.ShapeDtypeStruct((m, n), out_dtype),
        scratch_shapes=[pltpu.VMEM((tm, tn), _F32)],
        compiler_params=_params("parallel", "parallel", "arbitrary"),
    )(a, b)


def _ffn_up_kernel(h_ref, wg_ref, wu_ref, o_ref):
    h = h_ref[...]
    a = _dot(h, wg_ref[...])
    b = _dot(h, wu_ref[...])
    o_ref[...] = (a * jax.nn.sigmoid(a) * b).astype(o_ref.dtype)


def _ffn_up(h, wg, wu, tm, tn):
    m, kd = h.shape
    n = wg.shape[1]
    w_spec = pl.BlockSpec((kd, tn), lambda i, j: (0, j))
    return pl.pallas_call(
        _ffn_up_kernel,
        grid=(m // tm, n // tn),
        in_specs=[pl.BlockSpec((tm, kd), lambda i, j: (i, 0)), w_spec, w_spec],
        out_specs=pl.BlockSpec((tm, tn), lambda i, j: (i, j)),
        out_shape=jax.ShapeDtypeStruct((m, n), _BF16),
        compiler_params=_params("parallel", "parallel"),
    )(h, wg, wu)


def _merge_kernel(oa_ref, or_ref, oc_ref, wa_ref, wr_ref, wc_ref, ga_ref, gr_ref, gc_ref, o_ref):
    acc = jax.nn.sigmoid(ga_ref[...]) * _dot(oa_ref[...], wa_ref[...])
    acc += jax.nn.sigmoid(gr_ref[...]) * _dot(or_ref[...], wr_ref[...])
    acc += jax.nn.sigmoid(gc_ref[...]) * _dot(oc_ref[...], wc_ref[...])
    o_ref[...] = acc.astype(o_ref.dtype)


def _gated_merge(o_a, o_r, o_c, w_a, w_r, w_c, proj, tm, tn):
    m = o_a.shape[0]
    gate_blk = G_OFF // tn
    per_branch = D_MODEL // tn

    def act(width):
        return pl.BlockSpec((tm, width), lambda i, j: (i, 0))

    def wgt(width):
        return pl.BlockSpec((width, tn), lambda i, j: (0, j))

    def gate(b):
        return pl.BlockSpec((tm, tn), lambda i, j: (i, gate_blk + b * per_branch + j))

    return pl.pallas_call(
        _merge_kernel,
        grid=(m // tm, D_MODEL // tn),
        in_specs=[act(A_WIDTH), act(R_WIDTH), act(C_WIDTH), wgt(A_WIDTH), wgt(R_WIDTH), wgt(C_WIDTH),
                  gate(0), gate(1), gate(2)],
        out_specs=pl.BlockSpec((tm, tn), lambda i, j: (i, j)),
        out_shape=jax.ShapeDtypeStruct((m, D_MODEL), _BF16),
        compiler_params=_params("parallel", "parallel"),
    )(o_a, o_r, o_c, w_a, w_r, w_c, proj, proj, proj)


def _t5_bucket_np(dist):
    dist = np.asarray(dist, np.int32)
    ratio = np.log(np.maximum(dist, MAX_EXACT).astype(np.float32) / np.float32(MAX_EXACT)) \
        / np.float32(math.log(BUCKET_MAX_DIST / MAX_EXACT)) * np.float32(N_BUCKETS - MAX_EXACT)
    large = MAX_EXACT + ratio.astype(np.int32)
    return np.where(dist < MAX_EXACT, dist, np.minimum(large, N_BUCKETS - 1)).astype(np.int32)


def _band_buckets():
    qi = np.arange(A_BLOCK)[:, None]
    ki = np.arange(2 * A_BLOCK)[None, :]
    j = qi + A_BLOCK - ki
    out = []
    for window, dilation in A_GROUPS:
        nk = window // dilation
        bkt = _t5_bucket_np(np.clip(j, 0, nk) * dilation)
        out.append(np.where((j >= 0) & (j <= nk), bkt, -1))
    return np.stack(out).astype(np.int32)


def _decode_buckets():
    out = []
    for window, dilation in A_GROUPS:
        nk = window // dilation
        out.append(_t5_bucket_np((nk - np.arange(nk)) * dilation)[:, None])
    return np.stack(out).astype(np.int32)


def _bias_lookup(bkt, tab_ref, col):
    bias = jnp.full(bkt.shape, NEG_INF, _F32)
    for b in range(N_BUCKETS):
        bias = jnp.where(bkt == b, tab_ref[b, col], bias)
    return bias


def _attn_prompt_kernel(tab_ref, bkt_ref, q1, k1, v1, q2, k2, v2, q3, k3, v3, o_ref,
                        acc_ref, m_ref, den_ref, bias_ref, *, seq):
    h = pl.program_id(1)
    qkv = ((q1, k1, v1), (q2, k2, v2), (q3, k3, v3))
    scale = DH_A ** -0.5
    for gi, (window, dil) in enumerate(A_GROUPS):
        q_ref, k_ref, v_ref = qkv[gi]
        bias_ref[...] = _bias_lookup(bkt_ref[gi], tab_ref, gi * H_A + h)
        nb = max(seq // dil // A_BLOCK, 1)

        def rows(i, r):
            start = r + i * (A_BLOCK * dil)
            if dil == 1:
                return pl.ds(pl.multiple_of(start, A_BLOCK), A_BLOCK)
            return pl.ds(start, A_BLOCK, stride=dil)

        def body(idx, carry):
            r = idx // nb
            i = idx % nb
            cur = rows(i, r)
            q = (q_ref[cur, :] * scale).astype(_BF16)
            s_c = _dot_nt(q, k_ref[cur, :].astype(_BF16)) + bias_ref[:, A_BLOCK:]
            m = jnp.max(s_c, axis=-1, keepdims=True)
            if nb > 1:
                prev = rows(jnp.maximum(i - 1, 0), r)
                s_p = _dot_nt(q, k_ref[prev, :].astype(_BF16)) + bias_ref[:, :A_BLOCK]
                s_p = jnp.where(i > 0, s_p, NEG_INF)
                m = jnp.maximum(m, jnp.max(s_p, axis=-1, keepdims=True))
            p_c = jnp.exp(s_c - m)
            den = jnp.sum(p_c, axis=-1, keepdims=True)
            acc = _dot(p_c.astype(_BF16), v_ref[cur, :].astype(_BF16))
            if nb > 1:
                p_p = jnp.exp(s_p - m)
                den += jnp.sum(p_p, axis=-1, keepdims=True)
                acc += _dot(p_p.astype(_BF16), v_ref[prev, :].astype(_BF16))
            if gi > 0:
                m_old = m_ref[cur, :]
                m_new = jnp.maximum(m_old, m)
                w_old = jnp.exp(m_old - m_new)
                w_cur = jnp.exp(m - m_new)
                acc = w_old * acc_ref[cur, :] + w_cur * acc
                den = w_old * den_ref[cur, :] + w_cur * den
                m = m_new
            acc_ref[cur, :] = acc
            den_ref[cur, :] = den
            m_ref[cur, :] = m
            return carry

        lax.fori_loop(0, seq // A_BLOCK, body, 0)
    o_ref[...] = (acc_ref[...] / den_ref[...]).astype(o_ref.dtype)


def _attn_prompt(proj, rel_bias, bkt):
    b, s, _ = proj.shape

    def col(gi, j):
        return pl.BlockSpec((None, s, DH_A), lambda bi, hi: (bi, 0, gi * 3 * H_A + j * H_A + hi))

    in_specs = [pl.BlockSpec(memory_space=pltpu.SMEM),
                pl.BlockSpec(bkt.shape, lambda bi, hi: (0, 0, 0))]
    in_specs += [col(gi, j) for gi in range(N_A_GROUPS) for j in range(3)]
    return pl.pallas_call(
        functools.partial(_attn_prompt_kernel, seq=s),
        grid=(b, H_A),
        in_specs=in_specs,
        out_specs=pl.BlockSpec((None, s, DH_A), lambda bi, hi: (bi, 0, hi)),
        out_shape=jax.ShapeDtypeStruct((b, s, A_WIDTH), _BF16),
        scratch_shapes=[pltpu.VMEM((s, DH_A), _F32), pltpu.VMEM((s, 1), _F32), pltpu.VMEM((s, 1), _F32),
                        pltpu.VMEM((A_BLOCK, 2 * A_BLOCK), _F32)],
        compiler_params=_params("parallel", "parallel"),
    )(rel_bias, bkt, *([proj] * 9))


def _attn_decode_kernel(tab_ref, bkt_ref, q1, k1, v1, q2, k2, v2, q3, k3, v3,
                        kb1, vb1, kb2, vb2, kb3, vb3, o_ref):
    h = pl.program_id(1)
    new = ((q1, k1, v1), (q2, k2, v2), (q3, k3, v3))
    bufs = ((kb1, vb1), (kb2, vb2), (kb3, vb3))
    scale = DH_A ** -0.5
    m_run = den_run = acc_run = None
    for gi, (window, dil) in enumerate(A_GROUPS):
        q_ref, k_ref, v_ref = new[gi]
        kb_ref, vb_ref = bufs[gi]
        nk = window // dil
        q = q_ref[...] * scale
        rows = pl.ds(0, nk) if dil == 1 else pl.ds(0, nk, stride=dil)
        s_buf = jnp.sum(kb_ref[rows, :] * q, axis=-1, keepdims=True)
        s_buf += _bias_lookup(bkt_ref[gi], tab_ref, gi * H_A + h)
        s_new = jnp.sum(k_ref[...] * q, axis=-1, keepdims=True) + tab_ref[0, gi * H_A + h]
        m = jnp.maximum(jnp.max(s_buf, axis=0, keepdims=True), s_new)
        p_buf = jnp.exp(s_buf - m)
        p_new = jnp.exp(s_new - m)
        den = jnp.sum(p_buf, axis=0, keepdims=True) + p_new
        acc = jnp.sum(p_buf * vb_ref[rows, :], axis=0, keepdims=True) + p_new * v_ref[...]
        if gi == 0:
            m_run, den_run, acc_run = m, den, acc
        else:
            m_new = jnp.maximum(m_run, m)
            w_old = jnp.exp(m_run - m_new)
            w_cur = jnp.exp(m - m_new)
            acc_run = w_old * acc_run + w_cur * acc
            den_run = w_old * den_run + w_cur * den
            m_run = m_new
    o_ref[...] = (acc_run / den_run).astype(o_ref.dtype)


def _attn_decode(proj3, caches, layer, rel_bias, bkt):
    b = proj3.shape[0]

    def col(gi, j):
        return pl.BlockSpec((None, 1, DH_A), lambda bi, hi: (bi, 0, gi * 3 * H_A + j * H_A + hi))

    def buf(gi, j):
        lg = caches[gi].shape[2]
        return pl.BlockSpec((None, None, lg, DH_A), lambda bi, hi: (layer, bi, 0, j * H_A + hi))

    in_specs = [pl.BlockSpec(memory_space=pltpu.SMEM),
                pl.BlockSpec(bkt.shape, lambda bi, hi: (0, 0, 0))]
    in_specs += [col(gi, j) for gi in range(N_A_GROUPS) for j in range(3)]
    in_specs += [buf(gi, j) for gi in range(N_A_GROUPS) for j in range(2)]
    cache_args = [caches[gi] for gi in range(N_A_GROUPS) for _ in range(2)]
    return pl.pallas_call(
        _attn_decode_kernel,
        grid=(b, H_A),
        in_specs=in_specs,
        out_specs=pl.BlockSpec((None, 1, DH_A), lambda bi, hi: (bi, 0, hi)),
        out_shape=jax.ShapeDtypeStruct((b, 1, A_WIDTH), _BF16),
        compiler_params=_params("parallel", "parallel"),
    )(rel_bias, bkt, *([proj3] * 9), *cache_args)


def _rotary_tables(pos):
    inv_freq = 1.0 / (ROPE_BASE ** jnp.linspace(0.0, 1.0, DK_R // 2, dtype=_F32))
    ang = pos[:, None] * inv_freq[None, :]
    cos = jnp.repeat(jnp.cos(ang), 2, axis=-1)
    sin = jnp.stack([-jnp.sin(ang), jnp.sin(ang)], axis=-1).reshape(ang.shape[0], DK_R)
    return cos, sin


def _pair_swap(x):
    parts = []
    for c in range(x.shape[-1] // LANES):
        xc = x[:, c * LANES:(c + 1) * LANES]
        nxt = pltpu.roll(xc, LANES - 1, 1)
        prv = pltpu.roll(xc, 1, 1)
        lane = lax.broadcasted_iota(jnp.int32, xc.shape, 1)
        parts.append(jnp.where(lane % 2 == 0, nxt, prv))
    return jnp.concatenate(parts, axis=-1)


def _rotate(x, cos, sin_signed):
    return x * cos + _pair_swap(x) * sin_signed


def _head_norm_gate(o, g, gn):
    hn = o * lax.rsqrt(jnp.mean(o * o, axis=-1, keepdims=True) + EPS) * gn
    return g * jax.nn.sigmoid(g) * hn


def _ret_prompt_kernel(lg_ref, q_ref, k_ref, v_ref, g_ref, cos_ref, sin_ref, gn_ref, o_ref, s_ref, *, seq):
    c = RET_CHUNK
    lgam = lg_ref[pl.program_id(1)]
    ri = lax.broadcasted_iota(jnp.int32, (c, c), 0)
    ci = lax.broadcasted_iota(jnp.int32, (c, c), 1)
    diff = (ri - ci).astype(_F32)
    decay = jnp.where(diff >= 0, jnp.exp(jnp.maximum(diff, 0.0) * lgam), 0.0)
    i_col = lax.broadcasted_iota(jnp.int32, (c, 1), 0).astype(_F32)
    k_scale = jnp.exp((c - 1 - i_col) * lgam)
    q_scale = jnp.exp((i_col + 1.0) * lgam)
    chunk_decay = jnp.exp(c * lgam)
    s_ref[...] = jnp.zeros_like(s_ref)

    def body(n, carry):
        rows = pl.ds(pl.multiple_of(n * c, c), c)
        cos = cos_ref[rows, :]
        sin = sin_ref[rows, :]
        q = _rotate(q_ref[rows, :], cos, sin)
        k = _rotate(k_ref[rows, :], cos, sin) * DK_R ** -0.5
        v = v_ref[rows, :].astype(_BF16)
        state = s_ref[...]
        scores = _dot_nt(q.astype(_BF16), k.astype(_BF16)) * decay
        o = _dot(scores.astype(_BF16), v)
        o += _dot((q * q_scale).astype(_BF16), state.astype(_BF16))
        s_ref[...] = chunk_decay * state + _dot_tn((k * k_scale).astype(_BF16), v)
        o_ref[rows, :] = _head_norm_gate(o, g_ref[rows, :], gn_ref[...]).astype(o_ref.dtype)
        return carry

    lax.fori_loop(0, seq // c, body, 0)


def _ret_prompt(proj, log_gamma, cos, sin, gn):
    b, s, _ = proj.shape
    base = R_OFF // DK_R

    def col(j):
        return pl.BlockSpec((None, s, DK_R), lambda bi, hi: (bi, 0, base + j * H_R + hi))

    tab = pl.BlockSpec((s, DK_R), lambda bi, hi: (0, 0))
    return pl.pallas_call(
        functools.partial(_ret_prompt_kernel, seq=s),
        grid=(b, H_R),
        in_specs=[pl.BlockSpec(memory_space=pltpu.SMEM), col(0), col(1), col(2), col(3), tab, tab,
                  pl.BlockSpec((1, DV_R), lambda bi, hi: (0, hi))],
        out_specs=[pl.BlockSpec((None, s, DV_R), lambda bi, hi: (bi, 0, hi)),
                   pl.BlockSpec((None, None, DK_R, DV_R), lambda bi, hi: (bi, hi, 0, 0))],
        out_shape=[jax.ShapeDtypeStruct((b, s, R_WIDTH), _BF16),
                   jax.ShapeDtypeStruct((b, H_R, DK_R, DV_R), _F32)],
        compiler_params=_params("parallel", "parallel"),
    )(log_gamma, proj, proj, proj, proj, cos, sin, gn.reshape(1, R_WIDTH))


def _ret_decode_kernel(lg_ref, q_ref, k_ref, v_ref, g_ref, cos_ref, sin_ref, gn_ref, s0_ref, o_ref, s_ref):
    gamma = jnp.exp(lg_ref[pl.program_id(1)])
    cos = cos_ref[...]
    sin = sin_ref[...]
    q = _rotate(q_ref[...], cos, sin)
    k = _rotate(k_ref[...], cos, sin) * DK_R ** -0.5
    eye = (lax.broadcasted_iota(jnp.int32, (DK_R, DK_R), 0)
           == lax.broadcasted_iota(jnp.int32, (DK_R, DK_R), 1))
    k_col = jnp.sum(jnp.where(eye, k, 0.0), axis=-1, keepdims=True)
    q_col = jnp.sum(jnp.where(eye, q, 0.0), axis=-1, keepdims=True)
    state = gamma * s0_ref[...] + k_col * v_ref[...]
    s_ref[...] = state
    o = jnp.sum(q_col * state, axis=0, keepdims=True)
    o_ref[...] = _head_norm_gate(o, g_ref[...], gn_ref[...]).astype(o_ref.dtype)


def _ret_decode(proj3, state, layer, log_gamma, cos, sin, gn):
    b = proj3.shape[0]
    base = R_OFF // DK_R

    def col(j):
        return pl.BlockSpec((None, 1, DK_R), lambda bi, hi: (bi, 0, base + j * H_R + hi))

    tab = pl.BlockSpec((1, DK_R), lambda bi, hi: (0, 0))
    return pl.pallas_call(
        _ret_decode_kernel,
        grid=(b, H_R),
        in_specs=[pl.BlockSpec(memory_space=pltpu.SMEM), col(0), col(1), col(2), col(3), tab, tab,
                  pl.BlockSpec((1, DV_R), lambda bi, hi: (0, hi)),
                  pl.BlockSpec((None, None, None, DK_R, DV_R), lambda bi, hi: (layer, bi, hi, 0, 0))],
        out_specs=[pl.BlockSpec((None, 1, DV_R), lambda bi, hi: (bi, 0, hi)),
                   pl.BlockSpec((None, None, DK_R, DV_R), lambda bi, hi: (bi, hi, 0, 0))],
        out_shape=[jax.ShapeDtypeStruct((b, 1, R_WIDTH), _BF16),
                   jax.ShapeDtypeStruct((b, H_R, DK_R, DV_R), _F32)],
        compiler_params=_params("parallel", "parallel"),
    )(log_gamma, proj3, proj3, proj3, proj3, cos, sin, gn.reshape(1, R_WIDTH), state)


def _mem_attn_kernel(q_ref, k_ref, v_ref, o_ref, *, t):
    q = q_ref[...] * DH_C ** -0.5
    if t < 8:
        q = jnp.broadcast_to(q[:1], (8, DH_C))
    s = _dot_nt(q.astype(_BF16), k_ref[...].astype(_BF16))
    p = jnp.exp(s - jnp.max(s, axis=-1, keepdims=True))
    p = p / jnp.sum(p, axis=-1, keepdims=True)
    o = _dot(p.astype(_BF16), v_ref[...].astype(_BF16))
    o_ref[...] = o[:t].astype(o_ref.dtype)


def _mem_attn(proj, mem_kv, kv_lead):
    b, t, _ = proj.shape
    base = C_OFF // DH_C
    lead = tuple(kv_lead)
    kv_block = (None,) * (len(lead) + 1) + (N_MEM, DH_C)

    def kv(j):
        return pl.BlockSpec(kv_block, lambda bi, hi: lead + (bi, 0, j * H_C + hi))

    return pl.pallas_call(
        functools.partial(_mem_attn_kernel, t=t),
        grid=(b, H_C),
        in_specs=[pl.BlockSpec((None, t, DH_C), lambda bi, hi: (bi, 0, base + hi)), kv(0), kv(1)],
        out_specs=pl.BlockSpec((None, t, DH_C), lambda bi, hi: (bi, 0, hi)),
        out_shape=jax.ShapeDtypeStruct((b, t, C_WIDTH), _BF16),
        compiler_params=_params("parallel", "parallel"),
    )(proj, mem_kv, mem_kv)


def _cast_weights(p, l):
    pad_n = ((0, 0), (0, D_FF_PAD - D_FF))
    pad_k = ((0, D_FF_PAD - D_FF), (0, 0))
    w = {}
    for f in ('ffn1', 'ffn2'):
        w[f + '_gate'] = jnp.pad(p[f + '_w_gate'][l].astype(_BF16), pad_n)
        w[f + '_up'] = jnp.pad(p[f + '_w_up'][l].astype(_BF16), pad_n)
        w[f + '_down'] = jnp.pad(p[f + '_w_down'][l].astype(_BF16), pad_k)
    for name in ('w_in', 'w_mem_kv', 'w_br_a', 'w_br_r', 'w_br_c', 'w_out'):
        w[name] = p[name][l].astype(_BF16)
    return w


def _ffn(h, w, prefix, tm):
    hid = _ffn_up(h, w[prefix + '_gate'], w[prefix + '_up'], tm, 512)
    return _matmul(hid, w[prefix + '_down'], _F32, tm, 1024, D_FF_PAD // 4)


def _trunk(x, batch, t, pos0, p, weights, consts, caches, state, mem_kv_cache, mem):
    m = x.shape[0]
    prompt = caches is None
    tm = min(m, 1024)
    tm_ew = min(m, 256)
    pos = pos0 + jnp.arange(t, dtype=_F32)
    cos, sin = _rotary_tables(pos)
    log_gamma = jnp.log1p(-jnp.exp2(-5.0 - jnp.arange(H_R, dtype=_F32)))
    new_a = [[] for _ in A_GROUPS]
    new_ret, new_mem = [], []
    h = _norm(x, p['ffn1_pre_g'][0], tm_ew)
    for l in range(DEPTH):
        w = weights[l]
        y = _ffn(h, w, 'ffn1', tm)
        x, h = _resid_norm(x, y, p['ffn1_post_g'][l], 0.5, p['mix_pre_g'][l], tm_ew)
        proj = _matmul(h, w['w_in'], _F32, tm, 1024, D_MODEL)
        proj3 = proj[:batch * t].reshape(batch, t, N_IN)
        if prompt:
            hm = _norm(mem.reshape(batch * N_MEM, D_MODEL), p['mem_norm_g'][l], tm_ew)
            mem_kv = _matmul(hm, w['w_mem_kv'], _F32, 1024, 1024, D_MODEL).reshape(batch, N_MEM, 2 * C_WIDTH)
            new_mem.append(mem_kv.reshape(batch, N_MEM, 2, H_C, DH_C))
            o_a = _attn_prompt(proj3, p['rel_bias'], consts['band_bkt'])
            o_r, s_new = _ret_prompt(proj3, log_gamma, cos, sin, p['ret_gn_g'][l])
            o_c = _mem_attn(proj3, mem_kv, ())
            for gi, (window, _) in enumerate(A_GROUPS):
                keep = min(window, t)
                kv = proj3[:, t - keep:, gi * 3 * A_WIDTH + A_WIDTH:(gi + 1) * 3 * A_WIDTH]
                new_a[gi].append(kv.reshape(batch, keep, 2, H_A, DH_A))
        else:
            o_a = _attn_decode(proj3, caches, l, p['rel_bias'], consts['dec_bkt'])
            o_r, s_new = _ret_decode(proj3, state, l, log_gamma, cos, sin, p['ret_gn_g'][l])
            o_c = _mem_attn(proj3, mem_kv_cache, (l,))
            for gi in range(N_A_GROUPS):
                lg = caches[gi].shape[2]
                kv = proj3[:, :, gi * 3 * A_WIDTH + A_WIDTH:(gi + 1) * 3 * A_WIDTH]
                nbuf = jnp.concatenate([caches[gi][l][:, t:], kv], axis=1)
                new_a[gi].append(nbuf.reshape(batch, lg, 2, H_A, DH_A))
        new_ret.append(s_new)

        def rows(o):
            o = o.reshape(batch * t, o.shape[-1])
            return jnp.pad(o, ((0, m - batch * t), (0, 0)))

        merged = _gated_merge(rows(o_a), rows(o_r), rows(o_c), w['w_br_a'], w['w_br_r'], w['w_br_c'],
                              proj, tm, 512)
        mix = _matmul(merged, w['w_out'], _F32, tm, 1024, D_MODEL)
        x, h = _resid_norm(x, mix, p['mix_post_g'][l], 1.0, p['ffn2_pre_g'][l], tm_ew)
        y = _ffn(h, w, 'ffn2', tm)
        g_next = p['ffn1_pre_g'][l + 1] if l + 1 < DEPTH else None
        x, h = _resid_norm(x, y, p['ffn2_post_g'][l], 0.5, g_next, tm_ew)
    mem_out = jnp.stack(new_mem) if new_mem else None
    return x, [jnp.stack(a) for a in new_a], jnp.stack(new_ret), mem_out


def kernel(x_prompt, x_sample, cache_a1_kv, cache_a2_kv, cache_a3_kv, state_ret, cache_mem_kv, mem_prompt,
           ffn1_pre_g, ffn1_w_gate, ffn1_w_up, ffn1_w_down, ffn1_post_g, mix_pre_g, w_in, rel_bias, ret_gn_g,
           mem_norm_g, w_mem_kv, w_br_a, w_br_r, w_br_c, w_out, mix_post_g,
           ffn2_pre_g, ffn2_w_gate, ffn2_w_up, ffn2_w_down, ffn2_post_g):
    p = {
        'ffn1_pre_g': ffn1_pre_g, 'ffn1_w_gate': ffn1_w_gate, 'ffn1_w_up': ffn1_w_up,
        'ffn1_w_down': ffn1_w_down, 'ffn1_post_g': ffn1_post_g, 'mix_pre_g': mix_pre_g,
        'w_in': w_in, 'rel_bias': rel_bias, 'ret_gn_g': ret_gn_g, 'mem_norm_g': mem_norm_g,
        'w_mem_kv': w_mem_kv, 'w_br_a': w_br_a, 'w_br_r': w_br_r, 'w_br_c': w_br_c,
        'w_out': w_out, 'mix_post_g': mix_post_g, 'ffn2_pre_g': ffn2_pre_g,
        'ffn2_w_gate': ffn2_w_gate, 'ffn2_w_up': ffn2_w_up, 'ffn2_w_down': ffn2_w_down,
        'ffn2_post_g': ffn2_post_g,
    }
    batch, seq, _ = x_prompt.shape
    dec_batch, dec_seq, _ = x_sample.shape
    assert dec_seq == 1, "the decode kernels handle one new token per sequence"
    weights = [_cast_weights(p, l) for l in range(DEPTH)]
    consts = {'band_bkt': jnp.asarray(_band_buckets()), 'dec_bkt': jnp.asarray(_decode_buckets())}

    y_p, a_p, ret_p, mem_p = _trunk(x_prompt.reshape(batch * seq, D_MODEL), batch, seq, 0.0, p, weights, consts,
                                    None, None, None, mem_prompt)

    xs = jnp.pad(x_sample.reshape(dec_batch, D_MODEL), ((0, SAMPLE_ROWS - dec_batch), (0, 0)))
    caches = [c.reshape(c.shape[0], c.shape[1], c.shape[2], 2 * H_A * DH_A)
              for c in (cache_a1_kv, cache_a2_kv, cache_a3_kv)]
    mem_cache = cache_mem_kv.reshape(DEPTH, dec_batch, N_MEM, 2 * C_WIDTH)
    y_s, a_s, ret_s, _ = _trunk(xs, dec_batch, dec_seq, float(PAST_LEN), p, weights, consts,
                                caches, state_ret, mem_cache, None)

    return (y_p.reshape(batch, seq, D_MODEL), y_s[:dec_batch].reshape(dec_batch, dec_seq, D_MODEL),
            a_p[0], a_p[1], a_p[2], ret_p, mem_p, a_s[0], a_s[1], a_s[2], ret_s)
```

```python
import functools
import math

import numpy as np
import jax
import jax.numpy as jnp
from jax import lax
from jax.experimental import pallas as pl
from jax.experimental.pallas import tpu as pltpu

D_MODEL = 4096
DEPTH = 2
PAST_LEN = 16384
A_GROUPS = ((128, 1), (512, 4), (2048, 16))
N_A_GROUPS = len(A_GROUPS)
DH_A = 128
A_WIDTH = D_MODEL // 4
H_A = A_WIDTH // DH_A
A_BLOCK = 128
N_BUCKETS = 32
MAX_EXACT = N_BUCKETS // 2
BUCKET_MAX_DIST = 2048
DK_R = 256
DV_R = 256
R_WIDTH = D_MODEL // 2
H_R = R_WIDTH // DV_R
RET_CHUNK = 128
ROPE_BASE = 10000.0
N_MEM = 256
H_C = 4
C_WIDTH = D_MODEL // 4
DH_C = C_WIDTH // H_C
D_FF = ((8 * D_MODEL // 3 + 255) // 256) * 256
N_BRANCH = 3
A_COLS = N_A_GROUPS * 3 * A_WIDTH
R_COLS = 2 * H_R * DK_R + 2 * R_WIDTH
C_COLS = C_WIDTH
G_COLS = N_BRANCH * D_MODEL
N_IN = A_COLS + R_COLS + C_COLS + G_COLS
R_OFF = A_COLS
C_OFF = A_COLS + R_COLS
G_OFF = A_COLS + R_COLS + C_COLS
EPS = 1e-6
NEG_INF = -1e30

LANES = 128
SUBLANES = 8
VMEM_LIMIT_BYTES = 58 * 1024 * 1024

SAMPLE_BLOCK = 64
N_ROW_TILES = 4
EW_ROWS = 192
FF_TILE = 256
DOWN_TK = 1024
DOWN_K_TILES = -(-D_FF // DOWN_TK)
DOWN_TAIL = D_FF - (DOWN_K_TILES - 1) * DOWN_TK
assert 0 < DOWN_TAIL < DOWN_TK and DOWN_TAIL % LANES == 0 and DOWN_K_TILES > 2

_BF16 = jnp.bfloat16
_F32 = jnp.float32


def _params(*sem):
    return pltpu.CompilerParams(dimension_semantics=sem, vmem_limit_bytes=VMEM_LIMIT_BYTES)


def _dot(a, b):
    return jnp.dot(a, b, preferred_element_type=_F32)


def _dot_nt(a, b):
    return lax.dot_general(a, b, (((1,), (1,)), ((), ())), preferred_element_type=_F32)


def _dot_tn(a, b):
    return lax.dot_general(a, b, (((0,), (0,)), ((), ())), preferred_element_type=_F32)


def _rms(x, g):
    return x * lax.rsqrt(jnp.mean(x * x, axis=-1, keepdims=True) + EPS) * g


def _resident(block, index_map):
    return pl.BlockSpec(block, index_map, pipeline_mode=pl.Buffered(1))


def _norm_kernel(x_ref, g_ref, h_ref):
    h_ref[...] = _rms(x_ref[...], g_ref[...]).astype(h_ref.dtype)


def _norm(x, g, tm):
    m, d = x.shape
    return pl.pallas_call(
        _norm_kernel,
        grid=(m // tm,),
        in_specs=[pl.BlockSpec((tm, d), lambda i: (i, 0)), pl.BlockSpec((1, d), lambda i: (0, 0))],
        out_specs=pl.BlockSpec((tm, d), lambda i: (i, 0)),
        out_shape=jax.ShapeDtypeStruct((m, d), _BF16),
        compiler_params=_params("parallel"),
    )(x, g.reshape(1, d))


def _resid_norm_kernel(x_ref, y_ref, gp_ref, gn_ref, xo_ref, h_ref, *, scale):
    x = x_ref[...] + scale * _rms(y_ref[...], gp_ref[...])
    xo_ref[...] = x
    h_ref[...] = _rms(x, gn_ref[...]).astype(h_ref.dtype)


def _resid_kernel(x_ref, y_ref, gp_ref, xo_ref, *, scale):
    xo_ref[...] = x_ref[...] + scale * _rms(y_ref[...], gp_ref[...])


def _resid_norm(x, y, g_post, scale, g_next, tm):
    m, d = x.shape
    row = pl.BlockSpec((tm, d), lambda i: (i, 0))
    vec = pl.BlockSpec((1, d), lambda i: (0, 0))
    if g_next is None:
        return pl.pallas_call(
            functools.partial(_resid_kernel, scale=scale),
            grid=(m // tm,), in_specs=[row, row, vec], out_specs=row,
            out_shape=jax.ShapeDtypeStruct((m, d), _F32),
            compiler_params=_params("parallel"),
        )(x, y, g_post.reshape(1, d)), None
    return pl.pallas_call(
        functools.partial(_resid_norm_kernel, scale=scale),
        grid=(m // tm,), in_specs=[row, row, vec, vec], out_specs=[row, row],
        out_shape=[jax.ShapeDtypeStruct((m, d), _F32), jax.ShapeDtypeStruct((m, d), _BF16)],
        compiler_params=_params("parallel"),
    )(x, y, g_post.reshape(1, d), g_next.reshape(1, d))


def _mm_kernel(x_ref, w_ref, o_ref):
    o_ref[...] = _dot(x_ref[...], w_ref[...].astype(_BF16)).astype(o_ref.dtype)


def _matmul(x, w, layer, out_dtype, tm, tn):
    m, kd = x.shape
    n = w.shape[2]
    return pl.pallas_call(
        _mm_kernel,
        grid=(m // tm, n // tn),
        in_specs=[_resident((tm, kd), lambda i, j: (i, 0)),
                  pl.BlockSpec((None, kd, tn), lambda i, j: (layer, 0, j))],
        out_specs=pl.BlockSpec((tm, tn), lambda i, j: (i, j)),
        out_shape=jax.ShapeDtypeStruct((m, n), out_dtype),
        compiler_params=_params("parallel", "parallel"),
    )(x, w)


def _ffn_up_kernel(h_ref, wg_ref, wu_ref, o_ref):
    h = h_ref[...]
    a = _dot(h, wg_ref[...].astype(_BF16))
    b = _dot(h, wu_ref[...].astype(_BF16))
    o_ref[...] = (a * jax.nn.sigmoid(a) * b).astype(o_ref.dtype)


def _ffn_up(h, wg, wu, layer, tm):
    m, kd = h.shape
    w_spec = pl.BlockSpec((None, kd, FF_TILE), lambda i, j: (layer, 0, j))
    return pl.pallas_call(
        _ffn_up_kernel,
        grid=(m // tm, D_FF // FF_TILE),
        in_specs=[_resident((tm, kd), lambda i, j: (i, 0)), w_spec, w_spec],
        out_specs=pl.BlockSpec((tm, FF_TILE), lambda i, j: (i, j)),
        out_shape=jax.ShapeDtypeStruct((m, D_FF), _BF16),
        compiler_params=_params("parallel", "parallel"),
    )(h, wg, wu)


def _ffn_down_kernel(h_ref, w_ref, o_ref):
    k = pl.program_id(2)
    last = DOWN_K_TILES - 1

    @pl.when(k == 0)
    def _():
        o_ref[...] = _dot(h_ref[...], w_ref[...].astype(_BF16))

    @pl.when((k > 0) & (k < last))
    def _():
        o_ref[...] += _dot(h_ref[...], w_ref[...].astype(_BF16))

    @pl.when(k == last)
    def _():
        o_ref[...] += _dot(h_ref[:, :DOWN_TAIL], w_ref[:DOWN_TAIL, :].astype(_BF16))


def _ffn_down(hid, w, layer, tm, tn):
    m = hid.shape[0]
    n = w.shape[2]
    return pl.pallas_call(
        _ffn_down_kernel,
        grid=(m // tm, n // tn, DOWN_K_TILES),
        in_specs=[pl.BlockSpec((tm, DOWN_TK), lambda i, j, k: (i, k)),
                  pl.BlockSpec((None, DOWN_TK, tn), lambda i, j, k: (layer, k, j))],
        out_specs=pl.BlockSpec((tm, tn), lambda i, j, k: (i, j)),
        out_shape=jax.ShapeDtypeStruct((m, n), _F32),
        compiler_params=_params("parallel", "parallel", "arbitrary"),
    )(hid, w)


def _merge_kernel(oa_ref, or_ref, oc_ref, wa_ref, wr_ref, wc_ref, ga_ref, gr_ref, gc_ref, o_ref):
    acc = jax.nn.sigmoid(ga_ref[...]) * _dot(oa_ref[...], wa_ref[...].astype(_BF16))
    acc += jax.nn.sigmoid(gr_ref[...]) * _dot(or_ref[...], wr_ref[...].astype(_BF16))
    acc += jax.nn.sigmoid(gc_ref[...]) * _dot(oc_ref[...], wc_ref[...].astype(_BF16))
    o_ref[...] = acc.astype(o_ref.dtype)


def _gated_merge(o_a, o_r, o_c, w_a, w_r, w_c, layer, proj, tm, tn):
    m = o_a.shape[0]
    gate_blk = G_OFF // tn
    per_branch = D_MODEL // tn

    def act(width):
        return _resident((tm, width), lambda i, j: (i, 0))

    def wgt(width):
        return pl.BlockSpec((None, width, tn), lambda i, j: (layer, 0, j))

    def gate(b):
        return pl.BlockSpec((tm, tn), lambda i, j: (i, gate_blk + b * per_branch + j))

    return pl.pallas_call(
        _merge_kernel,
        grid=(m // tm, D_MODEL // tn),
        in_specs=[act(A_WIDTH), act(R_WIDTH), act(C_WIDTH), wgt(A_WIDTH), wgt(R_WIDTH), wgt(C_WIDTH),
                  gate(0), gate(1), gate(2)],
        out_specs=pl.BlockSpec((tm, tn), lambda i, j: (i, j)),
        out_shape=jax.ShapeDtypeStruct((m, D_MODEL), _BF16),
        compiler_params=_params("parallel", "parallel"),
    )(o_a, o_r, o_c, w_a, w_r, w_c, proj, proj, proj)


def _t5_bucket_np(dist):
    dist = np.asarray(dist, np.int32)
    ratio = np.log(np.maximum(dist, MAX_EXACT).astype(np.float32) / np.float32(MAX_EXACT)) \
        / np.float32(math.log(BUCKET_MAX_DIST / MAX_EXACT)) * np.float32(N_BUCKETS - MAX_EXACT)
    large = MAX_EXACT + ratio.astype(np.int32)
    return np.where(dist < MAX_EXACT, dist, np.minimum(large, N_BUCKETS - 1)).astype(np.int32)


def _band_buckets():
    qi = np.arange(A_BLOCK)[:, None]
    ki = np.arange(2 * A_BLOCK)[None, :]
    j = qi + A_BLOCK - ki
    out = []
    for window, dilation in A_GROUPS:
        nk = window // dilation
        bkt = _t5_bucket_np(np.clip(j, 0, nk) * dilation)
        out.append(np.where((j >= 0) & (j <= nk), bkt, -1))
    return np.stack(out).astype(np.int32)


def _decode_buckets():
    out = []
    for window, dilation in A_GROUPS:
        nk = window // dilation
        out.append(_t5_bucket_np((nk - np.arange(nk)) * dilation)[:, None])
    return np.stack(out).astype(np.int32)


def _bias_lookup(bkt, tab_ref, col):
    bias = jnp.full(bkt.shape, NEG_INF, _F32)
    for b in range(N_BUCKETS):
        bias = jnp.where(bkt == b, tab_ref[b, col], bias)
    return bias


def _attn_prompt_kernel(tab_ref, bkt_ref, q1, k1, v1, q2, k2, v2, q3, k3, v3, o_ref,
                        acc_ref, m_ref, den_ref, bias_ref, *, seq):
    h = pl.program_id(1)
    qkv = ((q1, k1, v1), (q2, k2, v2), (q3, k3, v3))
    scale = DH_A ** -0.5
    for gi, (window, dil) in enumerate(A_GROUPS):
        q_ref, k_ref, v_ref = qkv[gi]
        bias_ref[...] = _bias_lookup(bkt_ref[gi], tab_ref, gi * H_A + h)
        nb = max(seq // dil // A_BLOCK, 1)

        def rows(i, r):
            start = r + i * (A_BLOCK * dil)
            if dil == 1:
                return pl.ds(pl.multiple_of(start, A_BLOCK), A_BLOCK)
            return pl.ds(start, A_BLOCK, stride=dil)

        def body(idx, carry):
            r = idx // nb
            i = idx % nb
            cur = rows(i, r)
            q = (q_ref[cur, :] * scale).astype(_BF16)
            s_c = _dot_nt(q, k_ref[cur, :].astype(_BF16)) + bias_ref[:, A_BLOCK:]
            m = jnp.max(s_c, axis=-1, keepdims=True)
            if nb > 1:
                prev = rows(jnp.maximum(i - 1, 0), r)
                s_p = _dot_nt(q, k_ref[prev, :].astype(_BF16)) + bias_ref[:, :A_BLOCK]
                s_p = jnp.where(i > 0, s_p, NEG_INF)
                m = jnp.maximum(m, jnp.max(s_p, axis=-1, keepdims=True))
            p_c = jnp.exp(s_c - m)
            den = jnp.sum(p_c, axis=-1, keepdims=True)
            acc = _dot(p_c.astype(_BF16), v_ref[cur, :].astype(_BF16))
            if nb > 1:
                p_p = jnp.exp(s_p - m)
                den += jnp.sum(p_p, axis=-1, keepdims=True)
                acc += _dot(p_p.astype(_BF16), v_ref[prev, :].astype(_BF16))
            if gi > 0:
                m_old = m_ref[cur, :]
                m_new = jnp.maximum(m_old, m)
                w_old = jnp.exp(m_old - m_new)
                w_cur = jnp.exp(m - m_new)
                acc = w_old * acc_ref[cur, :] + w_cur * acc
                den = w_old * den_ref[cur, :] + w_cur * den
                m = m_new
            acc_ref[cur, :] = acc
            den_ref[cur, :] = den
            m_ref[cur, :] = m
            return carry

        lax.fori_loop(0, seq // A_BLOCK, body, 0, unroll=4)
    o_ref[...] = (acc_ref[...] / den_ref[...]).astype(o_ref.dtype)


def _attn_prompt(proj, batch, seq, rel_bias, bkt):
    def col(gi, j):
        return pl.BlockSpec((seq, DH_A), lambda bi, hi: (bi, gi * 3 * H_A + j * H_A + hi))

    in_specs = [pl.BlockSpec(memory_space=pltpu.SMEM),
                pl.BlockSpec(bkt.shape, lambda bi, hi: (0, 0, 0))]
    in_specs += [col(gi, j) for gi in range(N_A_GROUPS) for j in range(3)]
    return pl.pallas_call(
        functools.partial(_attn_prompt_kernel, seq=seq),
        grid=(batch, H_A),
        in_specs=in_specs,
        out_specs=pl.BlockSpec((seq, DH_A), lambda bi, hi: (bi, hi)),
        out_shape=jax.ShapeDtypeStruct((batch * seq, A_WIDTH), _BF16),
        scratch_shapes=[pltpu.VMEM((seq, DH_A), _F32), pltpu.VMEM((seq, 1), _F32), pltpu.VMEM((seq, 1), _F32),
                        pltpu.VMEM((A_BLOCK, 2 * A_BLOCK), _F32)],
        compiler_params=_params("parallel", "parallel"),
    )(rel_bias, bkt, *([proj] * 9))


def _write_sample_row(o_ref, rows_ref, value):
    b = pl.program_id(1)

    @pl.when(b == 0)
    def _():
        rows_ref[...] = jnp.zeros_like(rows_ref)

    rows_ref[pl.ds(b, 1), :] = value

    @pl.when(b == pl.num_programs(1) - 1)
    def _():
        o_ref[...] = rows_ref[...].astype(o_ref.dtype)


def _attn_decode_kernel(tab_ref, bkt_ref, q1, k1, v1, q2, k2, v2, q3, k3, v3,
                        kb1, vb1, kb2, vb2, kb3, vb3, o_ref, rows_ref):
    h = pl.program_id(0)
    b = pl.program_id(1)
    new = ((q1, k1, v1), (q2, k2, v2), (q3, k3, v3))
    bufs = ((kb1, vb1), (kb2, vb2), (kb3, vb3))
    scale = DH_A ** -0.5
    row = pl.ds(b, 1)
    m_run = den_run = acc_run = None
    for gi, (window, dil) in enumerate(A_GROUPS):
        q_ref, k_ref, v_ref = new[gi]
        kb_ref, vb_ref = bufs[gi]
        nk = window // dil
        q = q_ref[row, :] * scale
        rows = pl.ds(0, nk) if dil == 1 else pl.ds(0, nk, stride=dil)
        s_buf = jnp.sum(kb_ref[rows, :] * q, axis=-1, keepdims=True)
        s_buf += _bias_lookup(bkt_ref[gi], tab_ref, gi * H_A + h)
        s_new = jnp.sum(k_ref[row, :] * q, axis=-1, keepdims=True) + tab_ref[0, gi * H_A + h]
        m = jnp.maximum(jnp.max(s_buf, axis=0, keepdims=True), s_new)
        p_buf = jnp.exp(s_buf - m)
        p_new = jnp.exp(s_new - m)
        den = jnp.sum(p_buf, axis=0, keepdims=True) + p_new
        acc = jnp.sum(p_buf * vb_ref[rows, :], axis=0, keepdims=True) + p_new * v_ref[row, :]
        if gi == 0:
            m_run, den_run, acc_run = m, den, acc
        else:
            m_new = jnp.maximum(m_run, m)
            w_old = jnp.exp(m_run - m_new)
            w_cur = jnp.exp(m - m_new)
            acc_run = w_old * acc_run + w_cur * acc
            den_run = w_old * den_run + w_cur * den
            m_run = m_new
    _write_sample_row(o_ref, rows_ref, acc_run / den_run)


def _attn_decode(proj, row_blk, batch, caches, layer, rel_bias, bkt):
    def col(gi, j):
        return pl.BlockSpec((SUBLANES, DH_A), lambda hi, bi: (row_blk, gi * 3 * H_A + j * H_A + hi))

    def buf(gi, j):
        lg = caches[gi].shape[2]
        return pl.BlockSpec((None, None, lg, DH_A), lambda hi, bi: (layer, bi, 0, j * H_A + hi))

    in_specs = [pl.BlockSpec(memory_space=pltpu.SMEM),
                pl.BlockSpec(bkt.shape, lambda hi, bi: (0, 0, 0))]
    in_specs += [col(gi, j) for gi in range(N_A_GROUPS) for j in range(3)]
    in_specs += [buf(gi, j) for gi in range(N_A_GROUPS) for j in range(2)]
    cache_args = [caches[gi] for gi in range(N_A_GROUPS) for _ in range(2)]
    return pl.pallas_call(
        _attn_decode_kernel,
        grid=(H_A, batch),
        in_specs=in_specs,
        out_specs=pl.BlockSpec((SAMPLE_BLOCK, DH_A), lambda hi, bi: (0, hi)),
        out_shape=jax.ShapeDtypeStruct((SAMPLE_BLOCK, A_WIDTH), _BF16),
        scratch_shapes=[pltpu.VMEM((SAMPLE_BLOCK, DH_A), _F32)],
        compiler_params=_params("parallel", "arbitrary"),
    )(rel_bias, bkt, *([proj] * 9), *cache_args)


def _rotary_tables(pos):
    inv_freq = 1.0 / (ROPE_BASE ** jnp.linspace(0.0, 1.0, DK_R // 2, dtype=_F32))
    ang = pos[:, None] * inv_freq[None, :]
    cos = jnp.repeat(jnp.cos(ang), 2, axis=-1)
    sin = jnp.stack([-jnp.sin(ang), jnp.sin(ang)], axis=-1).reshape(ang.shape[0], DK_R)
    return cos, sin


def _pair_swap(x):
    parts = []
    for c in range(x.shape[-1] // LANES):
        xc = x[:, c * LANES:(c + 1) * LANES]
        nxt = pltpu.roll(xc, LANES - 1, 1)
        prv = pltpu.roll(xc, 1, 1)
        lane = lax.broadcasted_iota(jnp.int32, xc.shape, 1)
        parts.append(jnp.where(lane % 2 == 0, nxt, prv))
    return jnp.concatenate(parts, axis=-1)


def _rotate(x, cos, sin_signed):
    return x * cos + _pair_swap(x) * sin_signed


def _head_norm_gate(o, g, gn):
    hn = o * lax.rsqrt(jnp.mean(o * o, axis=-1, keepdims=True) + EPS) * gn
    return g * jax.nn.sigmoid(g) * hn


def _ret_prompt_kernel(lg_ref, q_ref, k_ref, v_ref, g_ref, cos_ref, sin_ref, gn_ref, o_ref, s_ref, *, seq):
    c = RET_CHUNK
    lgam = lg_ref[pl.program_id(1)]
    ri = lax.broadcasted_iota(jnp.int32, (c, c), 0)
    ci = lax.broadcasted_iota(jnp.int32, (c, c), 1)
    diff = (ri - ci).astype(_F32)
    decay = jnp.where(diff >= 0, jnp.exp(jnp.maximum(diff, 0.0) * lgam), 0.0)
    i_col = lax.broadcasted_iota(jnp.int32, (c, 1), 0).astype(_F32)
    k_scale = jnp.exp((c - 1 - i_col) * lgam)
    q_scale = jnp.exp((i_col + 1.0) * lgam)
    chunk_decay = jnp.exp(c * lgam)
    s_ref[...] = jnp.zeros_like(s_ref)

    def body(n, carry):
        rows = pl.ds(pl.multiple_of(n * c, c), c)
        cos = cos_ref[rows, :]
        sin = sin_ref[rows, :]
        q = _rotate(q_ref[rows, :], cos, sin)
        k = _rotate(k_ref[rows, :], cos, sin) * DK_R ** -0.5
        v = v_ref[rows, :].astype(_BF16)
        state = s_ref[...]
        scores = _dot_nt(q.astype(_BF16), k.astype(_BF16)) * decay
        o = _dot(scores.astype(_BF16), v)
        o += _dot((q * q_scale).astype(_BF16), state.astype(_BF16))
        s_ref[...] = chunk_decay * state + _dot_tn((k * k_scale).astype(_BF16), v)
        o_ref[rows, :] = _head_norm_gate(o, g_ref[rows, :], gn_ref[...]).astype(o_ref.dtype)
        return carry

    lax.fori_loop(0, seq // c, body, 0)


def _ret_prompt(proj, batch, seq, log_gamma, cos, sin, gn):
    base = R_OFF // DK_R

    def col(j):
        return pl.BlockSpec((seq, DK_R), lambda bi, hi: (bi, base + j * H_R + hi))

    tab = pl.BlockSpec((seq, DK_R), lambda bi, hi: (0, 0))
    return pl.pallas_call(
        functools.partial(_ret_prompt_kernel, seq=seq),
        grid=(batch, H_R),
        in_specs=[pl.BlockSpec(memory_space=pltpu.SMEM), col(0), col(1), col(2), col(3), tab, tab,
                  pl.BlockSpec((1, DV_R), lambda bi, hi: (0, hi))],
        out_specs=[pl.BlockSpec((seq, DV_R), lambda bi, hi: (bi, hi)),
                   pl.BlockSpec((None, None, DK_R, DV_R), lambda bi, hi: (bi, hi, 0, 0))],
        out_shape=[jax.ShapeDtypeStruct((batch * seq, R_WIDTH), _BF16),
                   jax.ShapeDtypeStruct((batch, H_R, DK_R, DV_R), _F32)],
        compiler_params=_params("parallel", "parallel"),
    )(log_gamma, proj, proj, proj, proj, cos, sin, gn.reshape(1, R_WIDTH))


def _ret_decode_kernel(lg_ref, q_ref, k_ref, v_ref, g_ref, cos_ref, sin_ref, gn_ref, s0_ref, o_ref, s_ref, rows_ref):
    b = pl.program_id(1)
    row = pl.ds(b, 1)
    gamma = jnp.exp(lg_ref[pl.program_id(0)])
    cos = cos_ref[...]
    sin = sin_ref[...]
    q = _rotate(q_ref[row, :], cos, sin)
    k = _rotate(k_ref[row, :], cos, sin) * DK_R ** -0.5
    eye = (lax.broadcasted_iota(jnp.int32, (DK_R, DK_R), 0)
           == lax.broadcasted_iota(jnp.int32, (DK_R, DK_R), 1))
    k_col = jnp.sum(jnp.where(eye, k, 0.0), axis=-1, keepdims=True)
    q_col = jnp.sum(jnp.where(eye, q, 0.0), axis=-1, keepdims=True)
    state = gamma * s0_ref[...] + k_col * v_ref[row, :]
    s_ref[...] = state
    o = jnp.sum(q_col * state, axis=0, keepdims=True)
    _write_sample_row(o_ref, rows_ref, _head_norm_gate(o, g_ref[row, :], gn_ref[...]))


def _ret_decode(proj, row_blk, batch, state, layer, log_gamma, cos, sin, gn):
    base = R_OFF // DK_R

    def col(j):
        return pl.BlockSpec((SUBLANES, DK_R), lambda hi, bi: (row_blk, base + j * H_R + hi))

    tab = pl.BlockSpec((1, DK_R), lambda hi, bi: (0, 0))
    return pl.pallas_call(
        _ret_decode_kernel,
        grid=(H_R, batch),
        in_specs=[pl.BlockSpec(memory_space=pltpu.SMEM), col(0), col(1), col(2), col(3), tab, tab,
                  pl.BlockSpec((1, DV_R), lambda hi, bi: (0, hi)),
                  pl.BlockSpec((None, None, None, DK_R, DV_R), lambda hi, bi: (layer, bi, hi, 0, 0))],
        out_specs=[pl.BlockSpec((SAMPLE_BLOCK, DV_R), lambda hi, bi: (0, hi)),
                   pl.BlockSpec((None, None, DK_R, DV_R), lambda hi, bi: (bi, hi, 0, 0))],
        out_shape=[jax.ShapeDtypeStruct((SAMPLE_BLOCK, R_WIDTH), _BF16),
                   jax.ShapeDtypeStruct((batch, H_R, DK_R, DV_R), _F32)],
        scratch_shapes=[pltpu.VMEM((SAMPLE_BLOCK, DV_R), _F32)],
        compiler_params=_params("parallel", "arbitrary"),
    )(log_gamma, proj, proj, proj, proj, cos, sin, gn.reshape(1, R_WIDTH), state)


def _softmax_rows(s):
    p = jnp.exp(s - jnp.max(s, axis=-1, keepdims=True))
    return p / jnp.sum(p, axis=-1, keepdims=True)


def _mem_prompt_kernel(q_ref, k_ref, v_ref, o_ref):
    q = (q_ref[...] * DH_C ** -0.5).astype(_BF16)
    p = _softmax_rows(_dot_nt(q, k_ref[...].astype(_BF16)))
    o_ref[...] = _dot(p.astype(_BF16), v_ref[...].astype(_BF16)).astype(o_ref.dtype)


def _mem_prompt(proj, batch, seq, mem_kv):
    base = C_OFF // DH_C

    def kv(j):
        return pl.BlockSpec((None, N_MEM, DH_C), lambda bi, hi: (bi, 0, j * H_C + hi))

    return pl.pallas_call(
        _mem_prompt_kernel,
        grid=(batch, H_C),
        in_specs=[pl.BlockSpec((seq, DH_C), lambda bi, hi: (bi, base + hi)), kv(0), kv(1)],
        out_specs=pl.BlockSpec((seq, DH_C), lambda bi, hi: (bi, hi)),
        out_shape=jax.ShapeDtypeStruct((batch * seq, C_WIDTH), _BF16),
        compiler_params=_params("parallel", "parallel"),
    )(proj, mem_kv, mem_kv)


def _mem_decode_kernel(q_ref, k_ref, v_ref, o_ref, rows_ref):
    b = pl.program_id(1)
    q = q_ref[pl.ds(b, 1), :] * DH_C ** -0.5
    q = jnp.broadcast_to(q, (SUBLANES, DH_C)).astype(_BF16)
    p = _softmax_rows(_dot_nt(q, k_ref[...].astype(_BF16)))
    o = _dot(p.astype(_BF16), v_ref[...].astype(_BF16))
    _write_sample_row(o_ref, rows_ref, o[:1])


def _mem_decode(proj, row_blk, batch, mem_kv, layer):
    base = C_OFF // DH_C

    def kv(j):
        return pl.BlockSpec((None, None, N_MEM, DH_C), lambda hi, bi: (layer, bi, 0, j * H_C + hi))

    return pl.pallas_call(
        _mem_decode_kernel,
        grid=(H_C, batch),
        in_specs=[pl.BlockSpec((SUBLANES, DH_C), lambda hi, bi: (row_blk, base + hi)), kv(0), kv(1)],
        out_specs=pl.BlockSpec((SAMPLE_BLOCK, DH_C), lambda hi, bi: (0, hi)),
        out_shape=jax.ShapeDtypeStruct((SAMPLE_BLOCK, C_WIDTH), _BF16),
        scratch_shapes=[pltpu.VMEM((SAMPLE_BLOCK, DH_C), _F32)],
        compiler_params=_params("parallel", "arbitrary"),
    )(proj, mem_kv, mem_kv)


def kernel(x_prompt, x_sample, cache_a1_kv, cache_a2_kv, cache_a3_kv, state_ret, cache_mem_kv, mem_prompt,
           ffn1_pre_g, ffn1_w_gate, ffn1_w_up, ffn1_w_down, ffn1_post_g, mix_pre_g, w_in, rel_bias, ret_gn_g,
           mem_norm_g, w_mem_kv, w_br_a, w_br_r, w_br_c, w_out, mix_post_g,
           ffn2_pre_g, ffn2_w_gate, ffn2_w_up, ffn2_w_down, ffn2_post_g):
    batch, seq, _ = x_prompt.shape
    dec_batch, dec_seq, _ = x_sample.shape
    assert dec_seq == 1, "the decode kernels handle one new token per sequence"
    assert dec_batch <= SUBLANES and seq % A_BLOCK == 0
    m_prompt = batch * seq
    m_all = m_prompt + SAMPLE_BLOCK
    assert m_prompt % SAMPLE_BLOCK == 0 and m_all % (N_ROW_TILES * 16) == 0 and m_all % EW_ROWS == 0
    tm = m_all // N_ROW_TILES
    sample_row_blk = m_prompt // SUBLANES

    band_bkt = jnp.asarray(_band_buckets())
    dec_bkt = jnp.asarray(_decode_buckets())
    log_gamma = jnp.log1p(-jnp.exp2(-5.0 - jnp.arange(H_R, dtype=_F32)))
    cos_p, sin_p = _rotary_tables(jnp.arange(seq, dtype=_F32))
    cos_s, sin_s = _rotary_tables(PAST_LEN + jnp.arange(dec_seq, dtype=_F32))
    caches = [c.reshape(c.shape[0], c.shape[1], c.shape[2], 2 * H_A * DH_A)
              for c in (cache_a1_kv, cache_a2_kv, cache_a3_kv)]
    mem_cache = cache_mem_kv.reshape(DEPTH, dec_batch, N_MEM, 2 * C_WIDTH)
    mem_rows = mem_prompt.reshape(batch * N_MEM, D_MODEL)

    x = jnp.concatenate([x_prompt.reshape(m_prompt, D_MODEL), x_sample.reshape(dec_batch, D_MODEL),
                         jnp.zeros((SAMPLE_BLOCK - dec_batch, D_MODEL), _F32)], axis=0)

    def ffn(h, wg, wu, wd, l):
        return _ffn_down(_ffn_up(h, wg, wu, l, tm), wd, l, tm, 1024)

    new_a_p = [[] for _ in A_GROUPS]
    new_a_s = [[] for _ in A_GROUPS]
    ret_p, ret_s, mem_p = [], [], []
    h = _norm(x, ffn1_pre_g[0], EW_ROWS)
    for l in range(DEPTH):
        y = ffn(h, ffn1_w_gate, ffn1_w_up, ffn1_w_down, l)
        x, h = _resid_norm(x, y, ffn1_post_g[l], 0.5, mix_pre_g[l], EW_ROWS)
        proj = _matmul(h, w_in, l, _F32, tm, 512)

        hm = _norm(mem_rows, mem_norm_g[l], 256)
        mem_kv = _matmul(hm, w_mem_kv, l, _F32, batch * N_MEM, 512).reshape(batch, N_MEM, 2 * C_WIDTH)
        mem_p.append(mem_kv.reshape(batch, N_MEM, 2, H_C, DH_C))

        oa_p = _attn_prompt(proj, batch, seq, rel_bias, band_bkt)
        or_p, s_p = _ret_prompt(proj, batch, seq, log_gamma, cos_p, sin_p, ret_gn_g[l])
        oc_p = _mem_prompt(proj, batch, seq, mem_kv)
        oa_s = _attn_decode(proj, sample_row_blk, dec_batch, caches, l, rel_bias, dec_bkt)
        or_s, s_s = _ret_decode(proj, sample_row_blk, dec_batch, state_ret, l, log_gamma, cos_s, sin_s, ret_gn_g[l])
        oc_s = _mem_decode(proj, sample_row_blk, dec_batch, mem_cache, l)
        ret_p.append(s_p)
        ret_s.append(s_s)

        proj_p = proj[:m_prompt].reshape(batch, seq, N_IN)
        proj_s = proj[m_prompt:m_prompt + dec_batch]
        for gi, (window, _) in enumerate(A_GROUPS):
            lo, hi = gi * 3 * A_WIDTH + A_WIDTH, (gi + 1) * 3 * A_WIDTH
            keep = min(window, seq)
            new_a_p[gi].append(proj_p[:, seq - keep:, lo:hi].reshape(batch, keep, 2, H_A, DH_A))
            lg = caches[gi].shape[2]
            nbuf = jnp.concatenate([caches[gi][l][:, dec_seq:], proj_s[:, None, lo:hi]], axis=1)
            new_a_s[gi].append(nbuf.reshape(dec_batch, lg, 2, H_A, DH_A))

        o_a = jnp.concatenate([oa_p, oa_s], axis=0)
        o_r = jnp.concatenate([or_p, or_s], axis=0)
        o_c = jnp.concatenate([oc_p, oc_s], axis=0)
        merged = _gated_merge(o_a, o_r, o_c, w_br_a, w_br_r, w_br_c, l, proj, tm, 256)
        mix = _matmul(merged, w_out, l, _F32, tm, 512)
        x, h = _resid_norm(x, mix, mix_post_g[l], 1.0, ffn2_pre_g[l], EW_ROWS)
        y = ffn(h, ffn2_w_gate, ffn2_w_up, ffn2_w_down, l)
        g_next = ffn1_pre_g[l + 1] if l + 1 < DEPTH else None
        x, h = _resid_norm(x, y, ffn2_post_g[l], 0.5, g_next, EW_ROWS)

    y_prompt = x[:m_prompt].reshape(batch, seq, D_MODEL)
    y_sample = x[m_prompt:m_prompt + dec_batch].reshape(dec_batch, dec_seq, D_MODEL)
    return (y_prompt, y_sample,
            jnp.stack(new_a_p[0]), jnp.stack(new_a_p[1]), jnp.stack(new_a_p[2]), jnp.stack(ret_p), jnp.stack(mem_p),
            jnp.stack(new_a_s[0]), jnp.stack(new_a_s[1]), jnp.stack(new_a_s[2]), jnp.stack(ret_s))
```
